```python
import jax, jax.numpy as jnp
from jax import lax
import numpy as np

D_MODEL = 1024
BATCH = 16
SEQ = 2048
DEPTH = 4
DEC_BATCH = 16
DEC_SEQ = 64
PAST_LEN = 2048

CHUNK = 64
D_PLE = 256
D_MIX = 2 * D_MODEL
C_A = D_MIX // 2
C_B = D_MIX - C_A
H_A = 8
DH_A = C_A // H_A
GMLP_CHUNK = 128
CONV_W = 31
EPS = 1e-6

kernel_name = "hymba_gmlp_conformer_stream_step"


def _rms_norm(x, g):
    xf = x.astype(jnp.float32)
    y = xf * lax.rsqrt(jnp.mean(xf * xf, axis=-1, keepdims=True) + EPS)
    return (y * g.astype(jnp.float32)).astype(x.dtype)


def _layer_norm(x, g, b):
    xf = x.astype(jnp.float32)
    xc = xf - jnp.mean(xf, axis=-1, keepdims=True)
    y = xc * lax.rsqrt(jnp.mean(xc * xc, axis=-1, keepdims=True) + EPS)
    return (y * g.astype(jnp.float32) + b.astype(jnp.float32)).astype(x.dtype)


def _masked_spatial(w_s):
    pos = jnp.arange(GMLP_CHUNK) // CHUNK
    mask = pos[:, None] >= pos[None, :]
    return jnp.where(mask[None], w_s, jnp.zeros((), w_s.dtype))


def _branch_inputs(h, g_pre, w_in, ln_v_g, ln_v_b):
    xn = _rms_norm(h, g_pre)
    proj = jnp.einsum('bsd,de->bse', xn, w_in)
    u, v, z_a, a_lin, a_gate, z_b = jnp.split(
        proj, [C_A, 2 * C_A, 3 * C_A, 3 * C_A + C_B, 3 * C_A + 2 * C_B], axis=-1)
    u = jax.nn.gelu(u)
    v = _layer_norm(jax.nn.gelu(v), ln_v_g, ln_v_b)
    a = a_lin * jax.nn.sigmoid(a_gate)
    return u, v, z_a, a, z_b


def _gmlp_chunks(u, v, w_s, b_s):
    bsz, s, _ = u.shape
    n_chunks = s // GMLP_CHUNK
    w_m = _masked_spatial(w_s)
    v5 = v.reshape(bsz, n_chunks, GMLP_CHUNK, H_A, DH_A)
    mixed = jnp.einsum('hts,bcshd->bcthd', w_m, v5) + b_s.T[:, :, None]
    return u * mixed.reshape(bsz, s, C_A)


def _gmlp_first_rows(u, v, w_s, b_s):
    bsz, t, _ = u.shape
    w_m = _masked_spatial(w_s)[:, :t, :t]
    v4 = v.reshape(bsz, t, H_A, DH_A)
    mixed = jnp.einsum('hts,bshd->bthd', w_m, v4) + b_s[:, :t].T[:, :, None]
    return u * mixed.reshape(bsz, t, C_A)


def _conv_tail(a_ext, conv_w, conv_b, ln_c_g, ln_c_b):
    c = lax.conv_general_dilated(
        a_ext, conv_w[:, None, :].astype(a_ext.dtype), window_strides=(1,), padding='VALID',
        dimension_numbers=('NWC', 'WIO', 'NWC'), feature_group_count=C_B) + conv_b
    return jax.nn.silu(_layer_norm(c, ln_c_g, ln_c_b))


def _mix_out(h, y_a, z_a, y_b, z_b, w_out, g_post, p_i, w_pe, w_pg):
    m = jnp.concatenate([y_a * jax.nn.silu(z_a), y_b * jax.nn.silu(z_b)], axis=-1)
    m = jnp.einsum('bse,ed->bsd', m, w_out)
    h = h + _rms_norm(m, g_post)
    gate = jax.nn.sigmoid(jnp.einsum('bsd,de->bse', h, w_pg))
    return h + gate * jnp.einsum('bsp,pd->bsd', p_i, w_pe)


def setup_inputs(seed: int = 0) -> dict:
    key = jax.random.key(seed)
    ks = jax.random.split(key, 24)
    f32 = jnp.float32
    nrm = lambda k, shape, s: jax.random.normal(k, shape, f32) * s
    return {
        "x_prompt": nrm(ks[0], (BATCH, SEQ, D_MODEL), 1.0),
        "x_sample": nrm(ks[1], (DEC_BATCH, DEC_SEQ, D_MODEL), 1.0),
        "state_conv": nrm(ks[2], (DEPTH, DEC_BATCH, CONV_W - 1, C_B), 0.5),
        "p_prompt": nrm(ks[3], (DEPTH, BATCH, SEQ, D_PLE), 1.0),
        "p_sample": nrm(ks[4], (DEPTH, DEC_BATCH, DEC_SEQ, D_PLE), 1.0),
        "g_pre": 1.0 + nrm(ks[5], (DEPTH, D_MODEL), 0.05),
        "w_in": nrm(ks[6], (DEPTH, D_MODEL, 3 * C_A + 3 * C_B), D_MODEL ** -0.5),
        "ln_v_g": 1.0 + nrm(ks[7], (DEPTH, C_A), 0.05),
        "ln_v_b": nrm(ks[8], (DEPTH, C_A), 0.02),
        "w_s": nrm(ks[9], (DEPTH, H_A, GMLP_CHUNK, GMLP_CHUNK), GMLP_CHUNK ** -0.5),
        "b_s": 1.0 + nrm(ks[10], (DEPTH, H_A, GMLP_CHUNK), 0.05),
        "conv_w": nrm(ks[11], (DEPTH, CONV_W, C_B), CONV_W ** -0.5),
        "conv_b": nrm(ks[12], (DEPTH, C_B), 0.02),
        "ln_c_g": 1.0 + nrm(ks[13], (DEPTH, C_B), 0.05),
        "ln_c_b": nrm(ks[14], (DEPTH, C_B), 0.02),
        "w_out": nrm(ks[15], (DEPTH, D_MIX, D_MODEL), D_MIX ** -0.5),
        "g_post": 1.0 + nrm(ks[16], (DEPTH, D_MODEL), 0.05),
        "w_pe": nrm(ks[17], (DEPTH, D_PLE, D_MODEL), D_PLE ** -0.5),
        "w_pg": nrm(ks[18], (DEPTH, D_MODEL, D_MODEL), D_MODEL ** -0.5),
    }


def reference(x_prompt, x_sample, state_conv, p_prompt, p_sample, g_pre, w_in, ln_v_g, ln_v_b,
              w_s, b_s, conv_w, conv_b, ln_c_g, ln_c_b, w_out, g_post, w_pe, w_pg):
    hp, hs = x_prompt, x_sample
    conv_p, conv_s, v_s = [], [], []
    for i in range(DEPTH):
        u, v, z_a, a, z_b = _branch_inputs(hp, g_pre[i], w_in[i], ln_v_g[i], ln_v_b[i])
        y_a = _gmlp_chunks(u, v, w_s[i], b_s[i])
        a_ext = jnp.pad(a, ((0, 0), (CONV_W - 1, 0), (0, 0)))
        y_b = _conv_tail(a_ext, conv_w[i], conv_b[i], ln_c_g[i], ln_c_b[i])
        conv_p.append(a[:, a.shape[1] - (CONV_W - 1):])
        hp = _mix_out(hp, y_a, z_a, y_b, z_b, w_out[i], g_post[i], p_prompt[i], w_pe[i], w_pg[i])

        u, v, z_a, a, z_b = _branch_inputs(hs, g_pre[i], w_in[i], ln_v_g[i], ln_v_b[i])
        y_a = _gmlp_first_rows(u, v, w_s[i], b_s[i])
        a_ext = jnp.concatenate([state_conv[i].astype(a.dtype), a], axis=1)
        y_b = _conv_tail(a_ext, conv_w[i], conv_b[i], ln_c_g[i], ln_c_b[i])
        conv_s.append(a_ext[:, a_ext.shape[1] - (CONV_W - 1):])
        v_s.append(v)
        hs = _mix_out(hs, y_a, z_a, y_b, z_b, w_out[i], g_post[i], p_sample[i], w_pe[i], w_pg[i])

    new_conv_prompt = jnp.stack(conv_p)
    new_conv_sample = jnp.stack(conv_s)
    new_gmlp_v_sample = jnp.stack(v_s)
    return (hp, hs, new_conv_prompt, new_conv_sample, new_gmlp_v_sample)
```

```python
import functools
import math

import jax
import jax.numpy as jnp
from jax import lax
from jax.experimental import pallas as pl
from jax.experimental.pallas import tpu as pltpu

D_MODEL = 1024
C_A = 1024
C_B = 1024
H_A = 8
DH_A = C_A // H_A
CHUNK = 64
GMLP_CHUNK = 128
CONV_W = 31
D_PLE = 256
EPS = 1e-6
N_PROJ = 3 * C_A + 3 * C_B

LANES = 128
HALO = 32
CONV_ROWS = 32
NORM_ROWS = 64
LANE_TILES = C_B // LANES
VMEM_LIMIT = 58 * 1024 * 1024
PIPE = 2

_BF16 = jnp.bfloat16
_F32 = jnp.float32


def _dot(a, b):
    return jnp.dot(a, b, preferred_element_type=_F32)


def _sigmoid(x):
    return 0.5 * jnp.tanh(0.5 * x) + 0.5


def _gelu(x):
    c0 = math.sqrt(2.0 / math.pi)
    c1 = c0 * 0.044715
    hx = 0.5 * x
    return hx + hx * jnp.tanh(x * (c0 + c1 * (x * x)))


def _rms_scale(x):
    return x * lax.rsqrt(jnp.mean(x * x, axis=-1, keepdims=True) + EPS)


def _layer_norm(x, g, b):
    xc = x - jnp.mean(x, axis=-1, keepdims=True)
    return xc * lax.rsqrt(jnp.mean(xc * xc, axis=-1, keepdims=True) + EPS) * g + b


def _layer_kernel(layer_ref, hin_ref, hres_ref, p_ref, cinit_ref, g_pre_ref, w_in_ref, lnv_g_ref,
                  lnv_b_ref, ws_ref, bs_ref, cw_ref, cb_ref, lnc_g_ref, lnc_b_ref, w_out_ref,
                  g_post_ref, w_pe_ref, w_pg_ref,
                  *rest, r, g, n_chunks, chunks_per_stream, emit_v):
    if emit_v:
        out_ref, conv_ref, v_ref, proj_s, m_s, vb_s, aext_s, wm_s, bias_s = rest
    else:
        out_ref, conv_ref, proj_s, m_s, vb_s, aext_s, wm_s, bias_s = rest
        v_ref = None
    t = pl.program_id(0)
    slot_p = t % 2
    slot_v = 1 - slot_p
    cv = t - 1
    cv_valid = (cv >= 0) & (cv < n_chunks)
    stream = jnp.clip(cv // chunks_per_stream, 0, n_chunks // chunks_per_stream - 1)

    @pl.when(t == 0)
    def _prep():
        proj_s[...] = jnp.zeros(proj_s.shape, _F32)
        m_s[...] = jnp.zeros(m_s.shape, _BF16)
        aext_s[...] = jnp.zeros(aext_s.shape, _F32)
        row = lax.broadcasted_iota(jnp.int32, (g, g), 0) // CHUNK
        col = lax.broadcasted_iota(jnp.int32, (g, g), 1) // CHUNK
        for hd in range(H_A):
            wm_s[hd] = jnp.where(row >= col, ws_ref[hd], 0.0).astype(_BF16)
            bias_s[hd] = jnp.broadcast_to(bs_ref[hd], (g, LANES))

    @pl.when(cv_valid & (cv % chunks_per_stream == 0))
    def _load_conv_state():
        for ct in range(LANE_TILES):
            aext_s[ct, 0:HALO, :] = cinit_ref[stream, :, ct * LANES:(ct + 1) * LANES]

    xn = (_rms_scale(hin_ref[...]) * g_pre_ref[...]).astype(_BF16)
    proj_s[slot_p] = _dot(xn, w_in_ref[...])

    def proj(rs, c0, width=C_A):
        return proj_s[slot_v, rs, c0:c0 + width]

    for rb in range(r // NORM_ROWS):
        rs = slice(rb * NORM_ROWS, (rb + 1) * NORM_ROWS)
        v = _layer_norm(_gelu(proj(rs, C_A)), lnv_g_ref[...], lnv_b_ref[...])
        if emit_v:
            v_ref[rs, :] = v
        vb_s[rs, :] = v.astype(_BF16)
    for gi in range(r // g):
        rs = slice(gi * g, (gi + 1) * g)
        for hd in range(H_A):
            c0 = hd * DH_A
            mixed = _dot(wm_s[hd], vb_s[rs, c0:c0 + DH_A]) + bias_s[hd]
            z_a = proj(rs, 2 * C_A + c0, DH_A)
            gate_a = _gelu(proj(rs, c0, DH_A)) * (z_a * _sigmoid(z_a))
            m_s[slot_v, rs, c0:c0 + DH_A] = (gate_a * mixed).astype(_BF16)

    o = 3 * C_A
    for ct in range(LANE_TILES):
        c0 = ct * LANES
        a = proj(slice(0, r), o + c0, LANES) * _sigmoid(proj(slice(0, r), o + C_B + c0, LANES))
        aext_s[ct, HALO:HALO + r, :] = a
    for rb in range(r // CONV_ROWS):
        rs = slice(rb * CONV_ROWS, (rb + 1) * CONV_ROWS)
        base = rb * CONV_ROWS + HALO - (CONV_W - 1)
        cols = []
        for ct in range(LANE_TILES):
            c0 = ct * LANES
            acc = cw_ref[0:1, c0:c0 + LANES] * aext_s[ct, base:base + CONV_ROWS, :]
            for k in range(1, CONV_W):
                acc = acc + cw_ref[k:k + 1, c0:c0 + LANES] * aext_s[ct, base + k:base + k + CONV_ROWS, :]
            cols.append(acc)
        cn = _layer_norm(jnp.concatenate(cols, axis=1) + cb_ref[...], lnc_g_ref[...], lnc_b_ref[...])
        z_b = proj(rs, o + 2 * C_B)
        m_s[slot_v, rs, C_A:C_A + C_B] = ((cn * _sigmoid(cn)) * (z_b * _sigmoid(z_b))).astype(_BF16)
    for ct in range(LANE_TILES):
        aext_s[ct, 0:HALO, :] = aext_s[ct, r:r + HALO, :]

    mo = _dot(m_s[slot_p], w_out_ref[...])
    h1 = hres_ref[...] + _rms_scale(mo) * g_post_ref[...]
    gate = _sigmoid(_dot(h1.astype(_BF16), w_pg_ref[...]))
    pe = _dot(p_ref[...].astype(_BF16), w_pe_ref[...])
    out_ref[...] = h1 + gate * pe

    @pl.when(cv_valid & (cv % chunks_per_stream == chunks_per_stream - 1))
    def _emit_conv_state():
        for ct in range(LANE_TILES):
            conv_ref[stream, :, ct * LANES:(ct + 1) * LANES] = aext_s[ct, 0:HALO, :]


def _layer_spec(shape):
    nd = len(shape) - 1
    return pl.BlockSpec((None,) + tuple(shape[1:]), lambda t, l: (l[0],) + (0,) * nd,
                        pipeline_mode=pl.Buffered(1))


def _layer_call(layer, h, p, cinit, params, *, r, emit_v, name):
    bsz, s, _ = h.shape
    g = params[4].shape[-1]
    assert s % r == 0 and r % g == 0 and r % CONV_ROWS == 0 and r % NORM_ROWS == 0 and r >= HALO
    n_chunks = bsz * s // r
    h2 = h.reshape(bsz * s, D_MODEL)

    def chunk_spec(width, lag):
        return pl.BlockSpec((r, width), lambda t, l: (jnp.clip(t - lag, 0, n_chunks - 1), 0))

    in_specs = [
        chunk_spec(D_MODEL, 0), chunk_spec(D_MODEL, PIPE),
        pl.BlockSpec((None, r, D_PLE), lambda t, l: (l[0], jnp.clip(t - PIPE, 0, n_chunks - 1), 0)),
        _layer_spec(cinit.shape),
    ] + [_layer_spec(w.shape) for w in params]
    out_shape = [jax.ShapeDtypeStruct((bsz * s, D_MODEL), _F32),
                 jax.ShapeDtypeStruct((bsz, HALO, C_B), _F32)]
    out_specs = [chunk_spec(D_MODEL, PIPE), pl.BlockSpec((bsz, HALO, C_B), lambda t, l: (0, 0, 0))]
    if emit_v:
        out_shape.append(jax.ShapeDtypeStruct((bsz * s, C_A), _F32))
        out_specs.append(chunk_spec(C_A, 1))
    scratch = [
        pltpu.VMEM((2, r, N_PROJ), _F32),
        pltpu.VMEM((2, r, C_A + C_B), _BF16),
        pltpu.VMEM((r, C_A), _BF16),
        pltpu.VMEM((LANE_TILES, HALO + r, LANES), _F32),
        pltpu.VMEM((H_A, g, g), _BF16),
        pltpu.VMEM((H_A, g, LANES), _F32),
    ]
    kern = functools.partial(_layer_kernel, r=r, g=g, n_chunks=n_chunks,
                             chunks_per_stream=s // r, emit_v=emit_v)
    outs = pl.pallas_call(
        kern,
        grid_spec=pltpu.PrefetchScalarGridSpec(
            num_scalar_prefetch=1, grid=(n_chunks + PIPE,), in_specs=in_specs,
            out_specs=out_specs, scratch_shapes=scratch),
        out_shape=out_shape, name=name,
        compiler_params=pltpu.CompilerParams(
            dimension_semantics=("arbitrary",), vmem_limit_bytes=VMEM_LIMIT),
    )(layer, h2, h2, p.reshape(p.shape[0], bsz * s, D_PLE), cinit, *params)
    outs = list(outs)
    outs[0] = outs[0].reshape(bsz, s, D_MODEL)
    if emit_v:
        outs[2] = outs[2].reshape(bsz, s, C_A)
    return outs


def kernel(x_prompt, x_sample, state_conv, p_prompt, p_sample, g_pre, w_in, ln_v_g, ln_v_b,
           w_s, b_s, conv_w, conv_b, ln_c_g, ln_c_b, w_out, g_post, w_pe, w_pg):
    depth = w_in.shape[0]
    bsz = x_prompt.shape[0]
    t_dec = x_sample.shape[1]
    pad = HALO - (CONV_W - 1)
    row = lambda w: w[:, None, :]

    def params(ws, bs):
        return (row(g_pre), w_in.astype(_BF16), row(ln_v_g), row(ln_v_b), ws, bs[..., None],
                conv_w, row(conv_b), row(ln_c_g), row(ln_c_b), w_out.astype(_BF16), row(g_post),
                w_pe.astype(_BF16), w_pg.astype(_BF16))

    params_p = params(w_s, b_s)
    params_s = params(w_s[:, :, :t_dec, :t_dec], b_s[:, :, :t_dec])
    zero_state = jnp.zeros((depth, bsz, HALO, C_B), _F32)
    state_pad = jnp.pad(state_conv, ((0, 0), (0, 0), (pad, 0), (0, 0)))

    def layer_step(carry, i):
        hp, hs = carry
        layer = jnp.reshape(i, (1,))
        hp, cp = _layer_call(layer, hp, p_prompt, zero_state, params_p,
                             r=256, emit_v=False, name="prompt_layer")
        hs, cs, vs = _layer_call(layer, hs, p_sample, state_pad, params_s,
                                 r=t_dec, emit_v=True, name="sample_layer")
        return (hp, hs), (cp[:, pad:], cs[:, pad:], vs)

    (hp, hs), (conv_p, conv_s, v_s) = lax.scan(
        layer_step, (x_prompt, x_sample), jnp.arange(depth, dtype=jnp.int32))
    return (hp, hs, conv_p, conv_s, v_s)
```
